```python
import jax
import jax.numpy as jnp
from jax import lax
import numpy as np

D_MODEL = 1024
BATCH = 16
SEQ = 4096
DEPTH = 1

M_HEADS = 4
M_HEAD_DIM = D_MODEL // M_HEADS
M_WIDTH = M_HEADS * M_HEAD_DIM
M_CHUNK = 64
CONV_WIDTH = 4
A_HEAD_DIM = 64
A_HEADS = D_MODEL // A_HEAD_DIM
A_KV_HEADS = max(1, A_HEADS // 8)
A_GROUP = A_HEADS // A_KV_HEADS
A_WIDTH = A_HEADS * A_HEAD_DIM
A_KV_WIDTH = A_KV_HEADS * A_HEAD_DIM
WINDOW = 128
A_BLOCK = WINDOW
ROPE_THETA = 500000.0
ROPE_DIM = A_HEAD_DIM // 4
D_FF = 256 * (-(-8 * D_MODEL // (3 * 256)))
EPS = 1e-6
IN_SIZES = (M_WIDTH, M_WIDTH, M_WIDTH, M_WIDTH, M_HEADS, M_HEADS,
            A_WIDTH, A_KV_WIDTH, A_KV_WIDTH, D_MODEL, D_MODEL)
IN_OFFSETS = tuple(int(o) for o in np.cumsum(IN_SIZES)[:-1])
N_IN = int(sum(IN_SIZES))

kernel_name = 'hybrid_mlstm_swa_sinks_gated'


def rms_norm(x, g):
    xf = x.astype(jnp.float32)
    y = xf * lax.rsqrt(jnp.mean(xf * xf, axis=-1, keepdims=True) + EPS)
    return (y * g.astype(jnp.float32)).astype(x.dtype)


def partial_rope(x, pos):
    half = ROPE_DIM // 2
    inv_freq = ROPE_THETA ** (-jnp.arange(half, dtype=jnp.float32) * (2.0 / ROPE_DIM))
    ang = pos.astype(jnp.float32)[:, None] * inv_freq[None, :]
    cos = jnp.cos(ang)[:, None, :]
    sin = jnp.sin(ang)[:, None, :]
    xr = x[..., :ROPE_DIM].astype(jnp.float32)
    x1, x2 = xr[..., :half], xr[..., half:]
    rot = jnp.concatenate([x1 * cos - x2 * sin, x2 * cos + x1 * sin], axis=-1).astype(x.dtype)
    return jnp.concatenate([rot, x[..., ROPE_DIM:]], axis=-1)


def causal_depthwise_conv(x, w, b):
    y = lax.conv_general_dilated(
        x, w[:, None, :].astype(x.dtype), window_strides=(1,),
        padding=((CONV_WIDTH - 1, 0),), dimension_numbers=('NWC', 'WIO', 'NWC'),
        feature_group_count=x.shape[-1])
    return y + b.astype(x.dtype)


def mlstm_chunkwise(q, k, v, i_pre, f_pre):
    B, S, H, d = q.shape
    L = M_CHUNK
    NC = S // L
    k = k * (d ** -0.5)
    log_f = jax.nn.log_sigmoid(f_pre)

    def to_chunks(t):
        return t.reshape(B, NC, L, H, d).transpose(1, 0, 3, 2, 4)

    def gate_chunks(t):
        return t.reshape(B, NC, L, H).transpose(1, 0, 3, 2)

    causal = jnp.tril(jnp.ones((L, L), dtype=bool))

    def step(carry, inp):
        C, n, m = carry
        qc, kc, vc, ic, fc = inp
        b = jnp.cumsum(fc, axis=-1)
        log_d = b[..., :, None] - b[..., None, :] + ic[..., None, :]
        log_d = jnp.where(causal, log_d, -jnp.inf)
        m_inter = b + m[..., None]
        m_t = jnp.maximum(m_inter, jnp.max(log_d, axis=-1))
        scores = jnp.einsum('bhtd,bhsd->bhts', qc, kc) * jnp.exp(log_d - m_t[..., None])
        inter_scale = jnp.exp(m_inter - m_t)
        num = (jnp.einsum('bhts,bhse->bhte', scores, vc)
               + inter_scale[..., None] * jnp.einsum('bhtd,bhde->bhte', qc, C))
        den = jnp.sum(scores, axis=-1) + inter_scale * jnp.einsum('bhtd,bhd->bht', qc, n)
        h = num / jnp.maximum(jnp.abs(den), jnp.exp(-m_t))[..., None]
        b_last = b[..., -1]
        w = b_last[..., None] - b + ic
        m_new = jnp.maximum(b_last + m, jnp.max(w, axis=-1))
        decay = jnp.exp(b_last + m - m_new)
        kw = kc * jnp.exp(w - m_new[..., None])[..., None]
        C = decay[..., None, None] * C + jnp.einsum('bhsd,bhse->bhde', kw, vc)
        n = decay[..., None] * n + jnp.sum(kw, axis=2)
        return (C, n, m_new), h

    init = (jnp.zeros((B, H, d, d), jnp.float32), jnp.zeros((B, H, d), jnp.float32),
            jnp.zeros((B, H), jnp.float32))
    _, h = lax.scan(step, init, (to_chunks(q), to_chunks(k), to_chunks(v),
                                 gate_chunks(i_pre), gate_chunks(log_f)))
    return h.transpose(1, 0, 3, 2, 4).reshape(B, S, H, d)


def swa_gqa_sinks(q, k, v, sinks):
    B, S, H, hd = q.shape
    NB = S // A_BLOCK
    qb = q.reshape(B, NB, A_BLOCK, A_KV_HEADS, A_GROUP, hd)

    def band(t):
        padded = jnp.pad(t, ((0, 0), (A_BLOCK, 0), (0, 0), (0, 0)))
        prev = padded[:, :S].reshape(B, NB, A_BLOCK, A_KV_HEADS, hd)
        cur = t.reshape(B, NB, A_BLOCK, A_KV_HEADS, hd)
        return jnp.concatenate([prev, cur], axis=2)

    kb, vb = band(k), band(v)
    logits = jnp.einsum('bnqkgd,bnskd->bnkgqs', qb, kb,
                        preferred_element_type=jnp.float32) * (hd ** -0.5)
    blk = jnp.arange(NB)[:, None] * A_BLOCK
    qpos = blk + jnp.arange(A_BLOCK)[None, :]
    kpos = blk - A_BLOCK + jnp.arange(2 * A_BLOCK)[None, :]
    rel = qpos[:, :, None] - kpos[:, None, :]
    mask = (rel >= 0) & (rel < WINDOW) & (kpos[:, None, :] >= 0)
    logits = jnp.where(mask[None, :, None, None], logits, -jnp.inf)
    sink = sinks.astype(jnp.float32).reshape(A_KV_HEADS, A_GROUP)[None, None, :, :, None]
    mx = jnp.maximum(jnp.max(logits, axis=-1), sink)
    p = jnp.exp(logits - mx[..., None])
    p = p / (jnp.sum(p, axis=-1) + jnp.exp(sink - mx))[..., None]
    out = jnp.einsum('bnkgqs,bnskd->bnqkgd', p.astype(v.dtype), vb)
    return out.reshape(B, S, H * hd)


def setup_inputs(seed: int = 0) -> dict:
    key = jax.random.key(seed)
    ks = jax.random.split(key, 20)
    f32 = jnp.float32
    nrm = lambda k, shape, scale: jax.random.normal(k, shape, f32) * scale
    gate_bias = jnp.concatenate([
        nrm(ks[4], (DEPTH, M_HEADS), 0.1),
        jnp.broadcast_to(jnp.linspace(3.0, 6.0, M_HEADS, dtype=f32), (DEPTH, M_HEADS))
        + nrm(ks[5], (DEPTH, M_HEADS), 0.01)], axis=-1)
    return {
        'x': nrm(ks[0], (BATCH, SEQ, D_MODEL), 1.0),
        'norm1_g': 1.0 + nrm(ks[1], (DEPTH, D_MODEL), 0.02),
        'w_in': nrm(ks[2], (DEPTH, D_MODEL, N_IN), D_MODEL ** -0.5),
        'conv_w': nrm(ks[3], (DEPTH, CONV_WIDTH, 2 * M_WIDTH), CONV_WIDTH ** -0.5),
        'conv_b': nrm(ks[6], (DEPTH, 2 * M_WIDTH), 0.01),
        'b_mgate': gate_bias,
        'm_norm_g': 1.0 + nrm(ks[7], (DEPTH, M_WIDTH), 0.02),
        'q_norm_g': 1.0 + nrm(ks[8], (DEPTH, A_HEAD_DIM), 0.02),
        'k_norm_g': 1.0 + nrm(ks[9], (DEPTH, A_HEAD_DIM), 0.02),
        'sinks': nrm(ks[10], (DEPTH, A_HEADS), 1.0),
        'b_merge': nrm(ks[11], (DEPTH, 2 * D_MODEL), 0.01),
        'w_branch_m': nrm(ks[12], (DEPTH, M_WIDTH, D_MODEL), M_WIDTH ** -0.5),
        'w_branch_a': nrm(ks[13], (DEPTH, A_WIDTH, D_MODEL), A_WIDTH ** -0.5),
        'w_out': nrm(ks[14], (DEPTH, D_MODEL, D_MODEL), D_MODEL ** -0.5),
        'norm2_g': 1.0 + nrm(ks[15], (DEPTH, D_MODEL), 0.02),
        'w_ffn_in': nrm(ks[16], (DEPTH, D_MODEL, 2 * D_FF), D_MODEL ** -0.5),
        'w_ffn_out': nrm(ks[17], (DEPTH, D_FF, D_MODEL), D_FF ** -0.5),
    }


def reference(x, norm1_g, w_in, conv_w, conv_b, b_mgate, m_norm_g, q_norm_g, k_norm_g,
              sinks, b_merge, w_branch_m, w_branch_a, w_out, norm2_g, w_ffn_in, w_ffn_out):
    B, S, _ = x.shape
    pos = jnp.arange(S)
    for l in range(DEPTH):
        h = rms_norm(x, norm1_g[l])
        proj = h @ w_in[l]
        qk = jax.nn.silu(causal_depthwise_conv(proj[..., :2 * M_WIDTH], conv_w[l], conv_b[l]))
        _, _, mv, mo, mi, mf, aq, ak, av, gm, ga = jnp.split(proj, IN_OFFSETS, axis=-1)
        mq, mk = qk[..., :M_WIDTH], qk[..., M_WIDTH:]
        heads_m = lambda t: t.reshape(B, S, M_HEADS, M_HEAD_DIM).astype(jnp.float32)
        gb = b_mgate[l].astype(jnp.float32)
        h_tilde = mlstm_chunkwise(heads_m(mq), heads_m(mk), heads_m(mv),
                                  mi.astype(jnp.float32) + gb[:M_HEADS],
                                  mf.astype(jnp.float32) + gb[M_HEADS:])
        h_m = jax.nn.sigmoid(heads_m(mo)) * h_tilde
        y_m = rms_norm(h_m, m_norm_g[l].reshape(M_HEADS, M_HEAD_DIM)).reshape(B, S, M_WIDTH).astype(x.dtype)
        q = aq.reshape(B, S, A_HEADS, A_HEAD_DIM)
        k = ak.reshape(B, S, A_KV_HEADS, A_HEAD_DIM)
        v = av.reshape(B, S, A_KV_HEADS, A_HEAD_DIM)
        q = partial_rope(rms_norm(q, q_norm_g[l]), pos)
        k = partial_rope(rms_norm(k, k_norm_g[l]), pos)
        y_a = swa_gqa_sinks(q, k, v, sinks[l])
        g_m = jax.nn.sigmoid(gm + b_merge[l, :D_MODEL])
        g_a = jax.nn.sigmoid(ga + b_merge[l, D_MODEL:])
        merged = g_m * (y_m @ w_branch_m[l]) + g_a * (y_a @ w_branch_a[l])
        x = x + merged @ w_out[l]
        gu = rms_norm(x, norm2_g[l]) @ w_ffn_in[l]
        x = x + (jax.nn.silu(gu[..., :D_FF]) * gu[..., D_FF:]) @ w_ffn_out[l]
    return x
```

```python
import functools

import numpy as np
import jax
import jax.numpy as jnp
from jax import lax
from jax.experimental import pallas as pl
from jax.experimental.pallas import tpu as pltpu

F32 = jnp.float32
BF16 = jnp.bfloat16

D_MODEL = 1024
M_HEADS = 4
M_HEAD_DIM = 256
M_WIDTH = M_HEADS * M_HEAD_DIM
CONV_WIDTH = 4
A_HEAD_DIM = 64
A_HEADS = 16
A_KV_HEADS = 2
A_GROUP = A_HEADS // A_KV_HEADS
A_WIDTH = A_HEADS * A_HEAD_DIM
WINDOW = 128
ROPE_THETA = 500000.0
ROPE_DIM = A_HEAD_DIM // 4
D_FF = 2816
EPS = 1e-6

LANES = 128
MXU_N = 256
CARRY_ROWS = 8
GATE_ROWS = 16
VMEM_LIMIT = 56 * 1024 * 1024

PROJ_TM = 512
MLSTM_TB = 1024
MLSTM_L = 256
SWA_TQ = 512
FFN_TM = 256
FFN_CHUNK = 256


def _const_spec(shape):
    zeros = (0,) * len(shape)
    return pl.BlockSpec(shape, lambda *_: zeros, pipeline_mode=pl.Buffered(1))


def _params(semantics):
    return pltpu.CompilerParams(dimension_semantics=semantics, vmem_limit_bytes=VMEM_LIMIT)


def _sigmoid(z):
    return 1.0 / (1.0 + jnp.exp(-z))


def _norm_rope(a, gain, cos, sin, bd):
    ss = jnp.dot((a * a).astype(BF16), bd, preferred_element_type=F32)
    an = a * lax.rsqrt(ss * (1.0 / A_HEAD_DIM) + EPS) * gain
    lane = lax.broadcasted_iota(jnp.int32, an.shape, 1) % A_HEAD_DIM
    half = ROPE_DIM // 2
    partner = jnp.where(lane < half, pltpu.roll(an, LANES - half, 1), pltpu.roll(an, half, 1))
    return an * cos + partner * sin


def _dup_heads(t, first):
    lane = lax.broadcasted_iota(jnp.int32, t.shape, 1)
    r = pltpu.roll(t, A_HEAD_DIM, 1)
    if first:
        return jnp.where(lane < A_HEAD_DIM, t, r)
    return jnp.where(lane < A_HEAD_DIM, r, t)


def _proj_kernel(x_ref, g1_ref, wqk_ref, wvo_ref, wgt_ref, waq_ref, wakv_ref, wmg_ref,
                 cw_ref, cb_ref, bg_ref, qg_ref, kg_ref, cos_ref, sin_ref, bd_ref, bm_ref,
                 qk_ref, v_ref, o_ref, gt_ref, aq_ref, akv_ref, mg_ref, pbuf, *, tm):
    @pl.when(pl.program_id(1) == 0)
    def _():
        pbuf[0:CARRY_ROWS, :] = jnp.zeros((CARRY_ROWS, 2 * M_WIDTH), F32)

    x = x_ref[...]
    ms = jnp.mean(x * x, axis=-1, keepdims=True)
    h = (x * lax.rsqrt(ms + EPS) * g1_ref[...]).astype(BF16)

    base = CARRY_ROWS
    for c in range(2 * M_WIDTH // MXU_N):
        cs = slice(c * MXU_N, (c + 1) * MXU_N)
        acc = jnp.dot(h, wqk_ref[:, cs], preferred_element_type=F32)
        pbuf[base:base + tm, cs] = acc
        cw = cw_ref[:, cs]
        y = cb_ref[:, cs] + cw[3:4] * acc
        for j in range(1, CONV_WIDTH):
            y = y + cw[3 - j:4 - j] * pbuf[base - j:base - j + tm, cs]
        y = y * _sigmoid(y)
        if c * MXU_N >= M_WIDTH:
            y = y * (M_HEAD_DIM ** -0.5)
        qk_ref[:, cs] = y.astype(BF16)
        pbuf[0:CARRY_ROWS, cs] = pbuf[tm:tm + CARRY_ROWS, cs]

    for c in range(M_WIDTH // MXU_N):
        cs = slice(c * MXU_N, (c + 1) * MXU_N)
        v_ref[:, cs] = jnp.dot(h, wvo_ref[:, cs], preferred_element_type=F32).astype(BF16)
    for c in range(M_WIDTH // MXU_N):
        cs = slice(c * MXU_N, (c + 1) * MXU_N)
        ws = slice(M_WIDTH + c * MXU_N, M_WIDTH + (c + 1) * MXU_N)
        o_ref[:, cs] = _sigmoid(jnp.dot(h, wvo_ref[:, ws], preferred_element_type=F32)).astype(BF16)

    gt = lax.dot_general(wgt_ref[...], h, (((1,), (1,)), ((), ())), preferred_element_type=F32)
    gt = gt + bg_ref[...]
    log_f = jnp.minimum(gt, 0.0) - jnp.log(1.0 + jnp.exp(-jnp.abs(gt)))
    row = lax.broadcasted_iota(jnp.int32, gt.shape, 0)
    gt_ref[...] = jnp.where(row < M_HEADS, gt, log_f)

    cos = cos_ref[...]
    sin = sin_ref[...]
    bd = bd_ref[...]
    qg = qg_ref[...] * (A_HEAD_DIM ** -0.5)
    for c in range(A_WIDTH // MXU_N):
        a = jnp.dot(h, waq_ref[:, c * MXU_N:(c + 1) * MXU_N], preferred_element_type=F32)
        for hf in range(MXU_N // LANES):
            ls = slice(c * MXU_N + hf * LANES, c * MXU_N + (hf + 1) * LANES)
            aq_ref[:, ls] = _norm_rope(a[:, hf * LANES:(hf + 1) * LANES], qg, cos, sin, bd).astype(BF16)

    kv = jnp.dot(h, wakv_ref[...], preferred_element_type=F32)
    kn = _norm_rope(kv[:, 0:LANES], kg_ref[...], cos, sin, bd)
    vv = kv[:, LANES:2 * LANES]
    akv_ref[:, 0 * LANES:1 * LANES] = _dup_heads(kn, True).astype(BF16)
    akv_ref[:, 1 * LANES:2 * LANES] = _dup_heads(kn, False).astype(BF16)
    akv_ref[:, 2 * LANES:3 * LANES] = _dup_heads(vv, True).astype(BF16)
    akv_ref[:, 3 * LANES:4 * LANES] = _dup_heads(vv, False).astype(BF16)

    for c in range(2 * D_MODEL // MXU_N):
        cs = slice(c * MXU_N, (c + 1) * MXU_N)
        z = jnp.dot(h, wmg_ref[:, cs], preferred_element_type=F32) + bm_ref[:, cs]
        mg_ref[:, cs] = _sigmoid(z).astype(BF16)


def _proj_call(x2, g1, wqk, wvo, wgt, waq, wakv, wmg, cw, cb, bg, qg, kg, cos_t, sin_t, bd, bm,
               batch, seq):
    tm = min(PROJ_TM, seq)
    ns = seq // tm
    T = batch * seq
    tok = lambda w: pl.BlockSpec((tm, w), lambda b, s: (b * ns + s, 0))
    in_specs = [
        tok(D_MODEL), _const_spec(g1.shape), _const_spec(wqk.shape), _const_spec(wvo.shape),
        _const_spec(wgt.shape), _const_spec(waq.shape), _const_spec(wakv.shape), _const_spec(wmg.shape),
        _const_spec(cw.shape), _const_spec(cb.shape), _const_spec(bg.shape), _const_spec(qg.shape),
        _const_spec(kg.shape),
        pl.BlockSpec((tm, LANES), lambda b, s: (s, 0)), pl.BlockSpec((tm, LANES), lambda b, s: (s, 0)),
        _const_spec(bd.shape), _const_spec(bm.shape),
    ]
    out_shape = [
        jax.ShapeDtypeStruct((T, 2 * M_WIDTH), BF16), jax.ShapeDtypeStruct((T, M_WIDTH), BF16),
        jax.ShapeDtypeStruct((T, M_WIDTH), BF16), jax.ShapeDtypeStruct((GATE_ROWS, T), F32),
        jax.ShapeDtypeStruct((T, A_WIDTH), BF16), jax.ShapeDtypeStruct((T, 4 * LANES), BF16),
        jax.ShapeDtypeStruct((T, 2 * D_MODEL), BF16),
    ]
    out_specs = [
        tok(2 * M_WIDTH), tok(M_WIDTH), tok(M_WIDTH),
        pl.BlockSpec((GATE_ROWS, tm), lambda b, s: (0, b * ns + s)),
        tok(A_WIDTH), tok(4 * LANES), tok(2 * D_MODEL),
    ]
    return pl.pallas_call(
        functools.partial(_proj_kernel, tm=tm),
        grid=(batch, ns), in_specs=in_specs, out_specs=out_specs, out_shape=out_shape,
        scratch_shapes=[pltpu.VMEM((tm + CARRY_ROWS, 2 * M_WIDTH), F32)],
        compiler_params=_params(("arbitrary", "arbitrary")), name="proj",
    )(x2, g1, wqk, wvo, wgt, waq, wakv, wmg, cw, cb, bg, qg, kg, cos_t, sin_t, bd, bm)


def _mlstm_kernel(q_ref, k_ref, v_ref, o_ref, gt_ref, ng_ref, y_ref, c_ref, n_ref, m_ref, *, L, nchunk):
    head = pl.program_id(1)

    @pl.when(pl.program_id(2) == 0)
    def _():
        c_ref[...] = jnp.zeros(c_ref.shape, F32)
        n_ref[...] = jnp.zeros(n_ref.shape, F32)
        m_ref[...] = jnp.zeros(m_ref.shape, F32)

    row = lax.broadcasted_iota(jnp.int32, (L, L), 0)
    col = lax.broadcasted_iota(jnp.int32, (L, L), 1)
    tri = col <= row
    eye = col == row
    ng = ng_ref[...]

    def chunk(ci, carry):
        r0 = pl.multiple_of(ci * L, L)
        rows = pl.ds(r0, L)
        q = q_ref[rows, :]
        k = k_ref[rows, :]
        v = v_ref[rows, :]
        ig = gt_ref[pl.ds(head, 1), rows]
        lf = gt_ref[pl.ds(M_HEADS + head, 1), rows]
        m_prev = m_ref[0:1, 0:1]

        b_col = jnp.sum(jnp.where(tri, lf, 0.0), axis=1, keepdims=True)
        b_row = jnp.sum(jnp.where(eye, b_col, 0.0), axis=0, keepdims=True)
        i_col = jnp.sum(jnp.where(eye, ig, 0.0), axis=1, keepdims=True)
        b_last = jnp.sum(lf, axis=1, keepdims=True)

        log_d = jnp.where(tri, b_col - b_row + ig, -jnp.inf)
        m_inter = b_col + m_prev
        m_t = jnp.maximum(m_inter, jnp.max(log_d, axis=1, keepdims=True))
        s = lax.dot_general(q, k, (((1,), (1,)), ((), ())), preferred_element_type=F32)
        p = s * jnp.exp(log_d - m_t)
        inter = jnp.exp(m_inter - m_t)
        num = (jnp.dot(p.astype(BF16), v, preferred_element_type=F32)
               + inter * jnp.dot(q, c_ref[...].astype(BF16), preferred_element_type=F32))
        qn = jnp.sum(q.astype(F32) * n_ref[...], axis=1, keepdims=True)
        den = jnp.sum(p, axis=1, keepdims=True) + inter * qn
        h_tilde = num / jnp.maximum(jnp.abs(den), jnp.exp(-m_t))

        hm = o_ref[rows, :].astype(F32) * h_tilde
        y = hm * lax.rsqrt(jnp.mean(hm * hm, axis=-1, keepdims=True) + EPS) * ng
        y_ref[rows, :] = y.astype(BF16)

        w_col = b_last - b_col + i_col
        w_row = b_last - b_row + ig
        m_new = jnp.maximum(b_last + m_prev, jnp.max(w_row, axis=1, keepdims=True))
        decay = jnp.exp(b_last + m_prev - m_new)
        kw = k.astype(F32) * jnp.exp(w_col - m_new)
        c_ref[...] = decay * c_ref[...] + lax.dot_general(
            kw.astype(BF16), v, (((0,), (0,)), ((), ())), preferred_element_type=F32)
        n_ref[...] = decay * n_ref[...] + jnp.sum(kw, axis=0, keepdims=True)
        m_ref[...] = jnp.broadcast_to(m_new, m_ref.shape)
        return carry

    lax.fori_loop(0, nchunk, chunk, 0)


def _mlstm_call(qk, v, o, gt, ng, batch, seq):
    tb = min(MLSTM_TB, seq)
    L = min(MLSTM_L, tb)
    nb = seq // tb
    T = batch * seq
    d = M_HEAD_DIM
    head_spec = lambda off: pl.BlockSpec((tb, d), lambda b, h, c: (b * nb + c, h + off))
    return pl.pallas_call(
        functools.partial(_mlstm_kernel, L=L, nchunk=tb // L),
        grid=(batch, M_HEADS, nb),
        in_specs=[head_spec(0), head_spec(M_HEADS), head_spec(0), head_spec(0),
                  pl.BlockSpec((GATE_ROWS, tb), lambda b, h, c: (0, b * nb + c)),
                  pl.BlockSpec((1, d), lambda b, h, c: (0, h))],
        out_specs=head_spec(0),
        out_shape=jax.ShapeDtypeStruct((T, M_WIDTH), BF16),
        scratch_shapes=[pltpu.VMEM((d, d), F32), pltpu.VMEM((1, d), F32), pltpu.VMEM((8, LANES), F32)],
        compiler_params=_params(("arbitrary", "arbitrary", "arbitrary")), name="mlstm",
    )(qk, qk, v, o, gt, ng)


def _swa_kernel(sink_ref, q_ref, kv_ref, kvp_ref, y_ref, *, nblk):
    first_step = pl.program_id(1) == 0
    W = WINDOW
    row = lax.broadcasted_iota(jnp.int32, (W, W), 0)
    col = lax.broadcasted_iota(jnp.int32, (W, W), 1)
    tri = col <= row
    lane = lax.broadcasted_iota(jnp.int32, (2 * W, LANES), 1)
    lo = lane < A_HEAD_DIM
    out_lo = lax.broadcasted_iota(jnp.int32, (W, LANES), 1) < A_HEAD_DIM
    pairs = A_GROUP // 2

    for j in range(nblk):
        qrows = slice(j * W, (j + 1) * W)
        if j == 0:
            band = jnp.concatenate([kvp_ref[...], kv_ref[0:W, :]], axis=0)
            prev_bias = jnp.where(first_step, -jnp.inf, 0.0)
        else:
            band = kv_ref[(j - 1) * W:(j + 1) * W, :]
            prev_bias = 0.0
        for g in range(A_KV_HEADS):
            kd = band[:, g * LANES:(g + 1) * LANES]
            vd = band[:, (A_KV_HEADS + g) * LANES:(A_KV_HEADS + g + 1) * LANES]
            zero = jnp.zeros_like(kd)
            k_par = (jnp.where(lo, kd, zero), jnp.where(lo, zero, kd))
            v_par = (jnp.where(lo, vd, zero), jnp.where(lo, zero, vd))
            q4 = jnp.concatenate(
                [q_ref[qrows, (g * pairs + p) * LANES:(g * pairs + p + 1) * LANES] for p in range(pairs)],
                axis=0)
            outs = []
            rinv = []
            for par in range(2):
                s = lax.dot_general(q4, k_par[par], (((1,), (1,)), ((), ())),
                                    preferred_element_type=F32)
                pb_list = []
                rv = []
                for p in range(pairs):
                    head = g * A_GROUP + 2 * p + par
                    sp = s[p * W:(p + 1) * W, :]
                    logit = jnp.where(tri, sp[:, W:], sp[:, :W] + prev_bias)
                    sink = sink_ref[head]
                    mx = jnp.maximum(jnp.max(logit, axis=1, keepdims=True), sink)
                    e = jnp.exp(logit - mx)
                    denom = jnp.sum(e, axis=1, keepdims=True) + jnp.exp(sink - mx)
                    rv.append(1.0 / denom)
                    pb_list.append(jnp.concatenate(
                        [jnp.where(tri, 0.0, e), jnp.where(tri, e, 0.0)], axis=1).astype(BF16))
                pb = jnp.concatenate(pb_list, axis=0)
                outs.append(jnp.dot(pb, v_par[par], preferred_element_type=F32))
                rinv.append(rv)
            for p in range(pairs):
                scale = jnp.where(out_lo, rinv[0][p], rinv[1][p])
                o_pair = (outs[0][p * W:(p + 1) * W, :] + outs[1][p * W:(p + 1) * W, :]) * scale
                y_ref[qrows, (g * pairs + p) * LANES:(g * pairs + p + 1) * LANES] = o_pair.astype(BF16)


def _swa_call(sinks, aq, akv, batch, seq):
    tq = min(SWA_TQ, seq)
    nq = seq // tq
    nblk = tq // WINDOW
    T = batch * seq
    return pl.pallas_call(
        functools.partial(_swa_kernel, nblk=nblk),
        grid=(batch, nq),
        in_specs=[pl.BlockSpec(memory_space=pltpu.SMEM),
                  pl.BlockSpec((tq, A_WIDTH), lambda b, s: (b * nq + s, 0)),
                  pl.BlockSpec((tq, 4 * LANES), lambda b, s: (b * nq + s, 0)),
                  pl.BlockSpec((WINDOW, 4 * LANES),
                               lambda b, s: (jnp.maximum((b * nq + s) * nblk - 1, 0), 0))],
        out_specs=pl.BlockSpec((tq, A_WIDTH), lambda b, s: (b * nq + s, 0)),
        out_shape=jax.ShapeDtypeStruct((T, A_WIDTH), BF16),
        compiler_params=_params(("arbitrary", "arbitrary")), name="swa",
    )(sinks, aq, akv, akv)


def _merge_ffn_kernel(x_ref, ym_ref, ya_ref, mg_ref, wbm_ref, wba_ref, wout_ref, g2_ref,
                      wfi_ref, wfo_ref, out_ref):
    m1 = jnp.dot(ym_ref[...], wbm_ref[...], preferred_element_type=F32)
    m2 = jnp.dot(ya_ref[...], wba_ref[...], preferred_element_type=F32)
    merged = mg_ref[:, 0:D_MODEL].astype(F32) * m1 + mg_ref[:, D_MODEL:2 * D_MODEL].astype(F32) * m2
    x1 = x_ref[...] + jnp.dot(merged.astype(BF16), wout_ref[...], preferred_element_type=F32)
    ms = jnp.mean(x1 * x1, axis=-1, keepdims=True)
    h2 = (x1 * lax.rsqrt(ms + EPS) * g2_ref[...]).astype(BF16)
    acc = x1
    for j in range(D_FF // FFN_CHUNK):
        gs = slice(j * FFN_CHUNK, (j + 1) * FFN_CHUNK)
        us = slice(D_FF + j * FFN_CHUNK, D_FF + (j + 1) * FFN_CHUNK)
        g = jnp.dot(h2, wfi_ref[:, gs], preferred_element_type=F32)
        u = jnp.dot(h2, wfi_ref[:, us], preferred_element_type=F32)
        a = (g * _sigmoid(g) * u).astype(BF16)
        acc = acc + jnp.dot(a, wfo_ref[gs, :], preferred_element_type=F32)
    out_ref[...] = acc


def _merge_ffn_call(x2, ym, ya, mg, wbm, wba, wout, g2, wfi, wfo):
    T = x2.shape[0]
    tm = min(FFN_TM, T)
    tok = lambda w: pl.BlockSpec((tm, w), lambda i: (i, 0))
    return pl.pallas_call(
        _merge_ffn_kernel,
        grid=(T // tm,),
        in_specs=[tok(D_MODEL), tok(M_WIDTH), tok(A_WIDTH), tok(2 * D_MODEL),
                  _const_spec(wbm.shape), _const_spec(wba.shape), _const_spec(wout.shape),
                  _const_spec(g2.shape), _const_spec(wfi.shape), _const_spec(wfo.shape)],
        out_specs=tok(D_MODEL),
        out_shape=jax.ShapeDtypeStruct((T, D_MODEL), F32),
        compiler_params=_params(("arbitrary",)), name="merge_ffn",
    )(x2, ym, ya, mg, wbm, wba, wout, g2, wfi, wfo)


def _rope_tables(seq):
    half = ROPE_DIM // 2
    inv_freq = ROPE_THETA ** (-jnp.arange(half, dtype=F32) * (2.0 / ROPE_DIM))
    ang = jnp.arange(seq).astype(F32)[:, None] * inv_freq[None, :]
    cos, sin = jnp.cos(ang), jnp.sin(ang)
    pad = A_HEAD_DIM - ROPE_DIM
    cos_h = jnp.concatenate([cos, cos, jnp.ones((seq, pad), F32)], axis=1)
    sin_h = jnp.concatenate([-sin, sin, jnp.zeros((seq, pad), F32)], axis=1)
    return jnp.tile(cos_h, (1, LANES // A_HEAD_DIM)), jnp.tile(sin_h, (1, LANES // A_HEAD_DIM))


def kernel(x, norm1_g, w_in, conv_w, conv_b, b_mgate, m_norm_g, q_norm_g, k_norm_g, sinks, b_merge,
           w_branch_m, w_branch_a, w_out, norm2_g, w_ffn_in, w_ffn_out):
    batch, seq, _ = x.shape
    depth = w_in.shape[0]
    cos_t, sin_t = _rope_tables(seq)
    blk = np.arange(LANES) // A_HEAD_DIM
    bd = jnp.asarray(blk[:, None] == blk[None, :], BF16)
    x2 = x.reshape(batch * seq, D_MODEL)
    o_gate = 4 * M_WIDTH
    o_aq = o_gate + 2 * M_HEADS
    o_ak = o_aq + A_WIDTH
    o_mg = o_ak + 2 * A_KV_HEADS * A_HEAD_DIM
    for l in range(depth):
        w = w_in[l]
        wgt = jnp.zeros((GATE_ROWS, D_MODEL), F32).at[0:2 * M_HEADS].set(w[:, o_gate:o_aq].T)
        bg = jnp.zeros((GATE_ROWS, 1), F32).at[0:2 * M_HEADS, 0].set(b_mgate[l])
        qk, mv, mo, gt, aq, akv, mg = _proj_call(
            x2, norm1_g[l][None, :], w[:, 0:2 * M_WIDTH].astype(BF16),
            w[:, 2 * M_WIDTH:o_gate].astype(BF16), wgt.astype(BF16),
            w[:, o_aq:o_ak].astype(BF16), w[:, o_ak:o_mg].astype(BF16), w[:, o_mg:].astype(BF16),
            conv_w[l], conv_b[l][None, :], bg,
            jnp.tile(q_norm_g[l], LANES // A_HEAD_DIM)[None, :],
            jnp.tile(k_norm_g[l], LANES // A_HEAD_DIM)[None, :],
            cos_t, sin_t, bd, b_merge[l][None, :], batch, seq)
        ym = _mlstm_call(qk, mv, mo, gt, m_norm_g[l][None, :], batch, seq)
        ya = _swa_call(sinks[l], aq, akv, batch, seq)
        x2 = _merge_ffn_call(
            x2, ym, ya, mg, w_branch_m[l].astype(BF16), w_branch_a[l].astype(BF16),
            w_out[l].astype(BF16), norm2_g[l][None, :], w_ffn_in[l].astype(BF16),
            w_ffn_out[l].astype(BF16))
    return x2.reshape(batch, seq, D_MODEL)
```

```python
import functools

import numpy as np
import jax
import jax.numpy as jnp
from jax import lax
from jax.experimental import pallas as pl
from jax.experimental.pallas import tpu as pltpu

F32 = jnp.float32
BF16 = jnp.bfloat16

D_MODEL = 1024
M_HEADS = 4
M_HEAD_DIM = 256
M_WIDTH = M_HEADS * M_HEAD_DIM
CONV_WIDTH = 4
A_HEAD_DIM = 64
A_HEADS = 16
A_KV_HEADS = 2
A_GROUP = A_HEADS // A_KV_HEADS
A_WIDTH = A_HEADS * A_HEAD_DIM
WINDOW = 128
ROPE_THETA = 500000.0
ROPE_DIM = A_HEAD_DIM // 4
D_FF = 2816
EPS = 1e-6

LANES = 128
MXU_N = 256
CARRY_ROWS = 8
GATE_ROWS = 16
VMEM_LIMIT = 56 * 1024 * 1024

PROJ_TM = 512
PROJ_CHUNK = 2 * MXU_N
MLSTM_TB = 1024
MLSTM_L = 256
SWA_TQ = 512
FFN_TM = 256
FFN_CHUNK = 256


def _const_spec(shape):
    zeros = (0,) * len(shape)
    return pl.BlockSpec(shape, lambda *_: zeros, pipeline_mode=pl.Buffered(1))


def _params(semantics):
    return pltpu.CompilerParams(dimension_semantics=semantics, vmem_limit_bytes=VMEM_LIMIT)


def _sigmoid(z):
    return 1.0 / (1.0 + jnp.exp(-z))


def _norm_rope(a, gain, cos, sin, bd):
    ss = jnp.dot((a * a).astype(BF16), bd, preferred_element_type=F32)
    an = a * lax.rsqrt(ss * (1.0 / A_HEAD_DIM) + EPS) * gain
    lane = lax.broadcasted_iota(jnp.int32, an.shape, 1) % A_HEAD_DIM
    half = ROPE_DIM // 2
    partner = jnp.where(lane < half, pltpu.roll(an, LANES - half, 1), pltpu.roll(an, half, 1))
    return an * cos + partner * sin


def _dup_heads(t, first):
    lane = lax.broadcasted_iota(jnp.int32, t.shape, 1)
    r = pltpu.roll(t, A_HEAD_DIM, 1)
    if first:
        return jnp.where(lane < A_HEAD_DIM, t, r)
    return jnp.where(lane < A_HEAD_DIM, r, t)


def _chunk_scan(t, pos, op, identity):
    shift = 1
    while shift < MLSTM_L:
        t = op(t, jnp.where(pos >= shift, pltpu.roll(t, shift, 1), identity))
        shift *= 2
    return t


def _proj_kernel(x_ref, g1_ref, wqk_ref, wvo_ref, wgt_ref, waq_ref, wakv_ref, wmg_ref,
                 cw_ref, cb_ref, bg_ref, qg_ref, kg_ref, cos_ref, sin_ref, bd_ref, bm_ref,
                 q_ref, qi_ref, kt_ref, kwt_ref, v_ref, o_ref, d_ref, grow_ref, gcol_ref,
                 aq_ref, akv_ref, mg_ref, pbuf, mc_ref, ybuf, *, tm):
    @pl.when(pl.program_id(1) == 0)
    def _():
        pbuf[:, 0:CARRY_ROWS, :] = jnp.zeros((pbuf.shape[0], CARRY_ROWS, LANES), F32)
        mc_ref[...] = jnp.zeros(mc_ref.shape, F32)

    x = x_ref[...]
    ms = jnp.mean(x * x, axis=-1, keepdims=True)
    h = (x * lax.rsqrt(ms + EPS) * g1_ref[...]).astype(BF16)

    gt = lax.dot_general(wgt_ref[...], h, (((1,), (1,)), ((), ())), preferred_element_type=F32)
    gt = gt + bg_ref[...]
    log_f = jnp.minimum(gt, 0.0) - jnp.log(1.0 + jnp.exp(-jnp.abs(gt)))
    lane = lax.broadcasted_iota(jnp.int32, gt.shape, 1)
    pos = lane % MLSTM_L
    b = pltpu.roll(_chunk_scan(log_f, pos, jnp.add, 0.0), GATE_ROWS - M_HEADS, 0)
    a = gt - b
    cm = _chunk_scan(a, pos, jnp.maximum, -jnp.inf)
    m_cur = mc_ref[:, 0:1]
    m_prev = jnp.zeros_like(gt)
    mu_last = jnp.zeros_like(gt)
    for c in range(tm // MLSTM_L):
        last = (c + 1) * MLSTM_L - 1
        in_chunk = lane // MLSTM_L == c
        mu_last_c = jnp.maximum(cm[:, last:last + 1], m_cur)
        m_prev = jnp.where(in_chunk, m_cur, m_prev)
        mu_last = jnp.where(in_chunk, mu_last_c, mu_last)
        m_cur = b[:, last:last + 1] + mu_last_c
    mc_ref[...] = jnp.broadcast_to(m_cur, mc_ref.shape)
    mu = jnp.maximum(cm, m_prev)
    inter = jnp.exp(m_prev - mu)
    floor = jnp.exp(-(b + mu))
    e_last = jnp.exp(a - mu_last)
    decay = jnp.exp(m_prev - mu_last)
    row = lax.broadcasted_iota(jnp.int32, gt.shape, 0)
    stacked = jnp.where(row < M_HEADS, mu,
                        jnp.where(row < 2 * M_HEADS, pltpu.roll(inter, M_HEADS, 0),
                                  jnp.where(row < 3 * M_HEADS, pltpu.roll(floor, 2 * M_HEADS, 0), 0.0)))
    gcol = jnp.concatenate([stacked, jnp.zeros((LANES - GATE_ROWS, tm), F32)], axis=0).T
    gcol_ref[...] = gcol
    grow_ref[...] = decay

    tri = (lax.broadcasted_iota(jnp.int32, (MLSTM_L, MLSTM_L), 1)
           <= lax.broadcasted_iota(jnp.int32, (MLSTM_L, MLSTM_L), 0))
    for hd in range(M_HEADS):
        for c in range(tm // MLSTM_L):
            rs = slice(c * MLSTM_L, (c + 1) * MLSTM_L)
            dmat = jnp.where(tri, jnp.exp(a[hd:hd + 1, rs] - gcol[rs, hd:hd + 1]), 0.0)
            d_ref[rs, hd * M_HEAD_DIM:(hd + 1) * M_HEAD_DIM] = dmat.astype(BF16)

    base = CARRY_ROWS
    d = M_HEAD_DIM

    def chunk_dot(w_ref, c):
        return lambda: jnp.dot(h, w_ref[:, c * PROJ_CHUNK:(c + 1) * PROJ_CHUNK], preferred_element_type=F32)

    def qk_epilogue(c, acc):
        for g in range(PROJ_CHUNK // LANES):
            pbuf[c * (PROJ_CHUNK // LANES) + g, base:base + tm, :] = acc[:, g * LANES:(g + 1) * LANES]
        for sub in range(PROJ_CHUNK // d):
            hd = (c * PROJ_CHUNK) // d + sub
            hs = slice(hd * d, (hd + 1) * d)
            for g in range(d // LANES):
                pg = hd * (d // LANES) + g
                ls = slice(hd * d + g * LANES, hd * d + (g + 1) * LANES)
                cw = cw_ref[:, ls]
                cb = cb_ref[:, ls]
                taps = {e: pbuf[pg, pl.ds(base + e, tm // 8, stride=8), :] for e in range(1 - CONV_WIDTH, 8)}
                for e in range(8):
                    y = cb + cw[CONV_WIDTH - 1:CONV_WIDTH] * taps[e]
                    for j in range(1, CONV_WIDTH):
                        y = y + cw[CONV_WIDTH - 1 - j:CONV_WIDTH - j] * taps[e - j]
                    ybuf[g, pl.ds(e, tm // 8, stride=8), :] = y * _sigmoid(y)
                pbuf[pg, 0:CARRY_ROWS, :] = pbuf[pg, tm:tm + CARRY_ROWS, :]
            y = jnp.concatenate([ybuf[g] for g in range(d // LANES)], axis=1)
            if hd < M_HEADS:
                q_ref[:, hs] = y.astype(BF16)
                qi_ref[:, hs] = (y * gcol[:, M_HEADS + hd:M_HEADS + hd + 1]).astype(BF16)
            else:
                kh = hd - M_HEADS
                yt = (y * (d ** -0.5)).T
                kt_ref[kh * d:(kh + 1) * d, :] = yt.astype(BF16)
                kwt_ref[kh * d:(kh + 1) * d, :] = (yt * e_last[kh:kh + 1, :]).astype(BF16)

    def vo_epilogue(c, r):
        if c * PROJ_CHUNK < M_WIDTH:
            v_ref[:, c * PROJ_CHUNK:(c + 1) * PROJ_CHUNK] = r.astype(BF16)
        else:
            o_ref[:, c * PROJ_CHUNK - M_WIDTH:(c + 1) * PROJ_CHUNK - M_WIDTH] = _sigmoid(r).astype(BF16)

    cos = cos_ref[...]
    sin = sin_ref[...]
    bd = bd_ref[...]
    qg = qg_ref[...] * (A_HEAD_DIM ** -0.5)

    def aq_epilogue(c, r):
        for hf in range(PROJ_CHUNK // LANES):
            ls = slice(c * PROJ_CHUNK + hf * LANES, c * PROJ_CHUNK + (hf + 1) * LANES)
            aq_ref[:, ls] = _norm_rope(r[:, hf * LANES:(hf + 1) * LANES], qg, cos, sin, bd).astype(BF16)

    def akv_epilogue(kv):
        kn = _norm_rope(kv[:, 0:LANES], kg_ref[...], cos, sin, bd)
        vv = kv[:, LANES:2 * LANES]
        akv_ref[:, 0 * LANES:1 * LANES] = _dup_heads(kn, True).astype(BF16)
        akv_ref[:, 1 * LANES:2 * LANES] = _dup_heads(kn, False).astype(BF16)
        akv_ref[:, 2 * LANES:3 * LANES] = _dup_heads(vv, True).astype(BF16)
        akv_ref[:, 3 * LANES:4 * LANES] = _dup_heads(vv, False).astype(BF16)

    def mg_epilogue(c, r):
        cs = slice(c * PROJ_CHUNK, (c + 1) * PROJ_CHUNK)
        mg_ref[:, cs] = _sigmoid(r + bm_ref[:, cs]).astype(BF16)

    P = functools.partial
    qk = [(chunk_dot(wqk_ref, c), P(qk_epilogue, c)) for c in range(2 * M_WIDTH // PROJ_CHUNK)]
    vo = [(chunk_dot(wvo_ref, c), P(vo_epilogue, c)) for c in range(2 * M_WIDTH // PROJ_CHUNK)]
    aq = [(chunk_dot(waq_ref, c), P(aq_epilogue, c)) for c in range(A_WIDTH // PROJ_CHUNK)]
    mg = [(chunk_dot(wmg_ref, c), P(mg_epilogue, c)) for c in range(2 * D_MODEL // PROJ_CHUNK)]
    akv = [(lambda: jnp.dot(h, wakv_ref[...], preferred_element_type=F32), akv_epilogue)]
    stages = [qk[0], vo[0], qk[1], vo[1], qk[2], vo[2], qk[3], vo[3],
              aq[0], mg[0], aq[1], mg[1], akv[0], mg[2], mg[3]]
    pending = stages[0][0]()
    for k, (_, epilogue) in enumerate(stages):
        following = stages[k + 1][0]() if k + 1 < len(stages) else None
        epilogue(pending)
        pending = following


def _proj_call(x2, g1, wqk, wvo, wgt, waq, wakv, wmg, cw, cb, bg, qg, kg, cos_t, sin_t, bd, bm,
               batch, seq):
    tm = min(PROJ_TM, seq)
    ns = seq // tm
    T = batch * seq
    tok = lambda w: pl.BlockSpec((tm, w), lambda b, s: (b * ns + s, 0))
    in_specs = [
        tok(D_MODEL), _const_spec(g1.shape), _const_spec(wqk.shape), _const_spec(wvo.shape),
        _const_spec(wgt.shape), _const_spec(waq.shape), _const_spec(wakv.shape), _const_spec(wmg.shape),
        _const_spec(cw.shape), _const_spec(cb.shape), _const_spec(bg.shape), _const_spec(qg.shape),
        _const_spec(kg.shape),
        pl.BlockSpec((tm, LANES), lambda b, s: (s, 0)), pl.BlockSpec((tm, LANES), lambda b, s: (s, 0)),
        _const_spec(bd.shape), _const_spec(bm.shape),
    ]
    tok_t = lambda r: pl.BlockSpec((r, tm), lambda b, s: (0, b * ns + s))
    rowmajor = lambda w, dt=BF16: (jax.ShapeDtypeStruct((T, w), dt), tok(w))
    colmajor = lambda r, dt=BF16: (jax.ShapeDtypeStruct((r, T), dt), tok_t(r))
    outs = [rowmajor(M_WIDTH), rowmajor(M_WIDTH), colmajor(M_WIDTH), colmajor(M_WIDTH),
            rowmajor(M_WIDTH), rowmajor(M_WIDTH), rowmajor(M_WIDTH),
            colmajor(GATE_ROWS, F32), rowmajor(LANES, F32),
            rowmajor(A_WIDTH), rowmajor(4 * LANES), rowmajor(2 * D_MODEL)]
    out_shape = [o[0] for o in outs]
    out_specs = [o[1] for o in outs]
    return pl.pallas_call(
        functools.partial(_proj_kernel, tm=tm),
        grid=(batch, ns), in_specs=in_specs, out_specs=out_specs, out_shape=out_shape,
        scratch_shapes=[pltpu.VMEM((2 * M_WIDTH // LANES, tm + CARRY_ROWS, LANES), F32),
                        pltpu.VMEM((GATE_ROWS, LANES), F32),
                        pltpu.VMEM((M_HEAD_DIM // LANES, tm, LANES), F32)],
        compiler_params=_params(("arbitrary", "arbitrary")), name="proj",
    )(x2, g1, wqk, wvo, wgt, waq, wakv, wmg, cw, cb, bg, qg, kg, cos_t, sin_t, bd, bm)


def _mlstm_kernel(q_ref, qi_ref, kt_ref, kwt_ref, v_ref, d_ref, grow_ref, gcol_ref, y_ref, c_ref, *, L, nchunk):
    d = M_HEAD_DIM

    @pl.when(pl.program_id(1) == 0)
    def _():
        c_ref[...] = jnp.zeros(c_ref.shape, F32)

    ones = jnp.ones((L, LANES), BF16)

    def chunk(ci, carry):
        r0 = pl.multiple_of(ci * L, L)
        rows = pl.ds(r0, L)
        for h in range(M_HEADS):
            hs = slice(h * d, (h + 1) * d)
            vaug = jnp.concatenate([v_ref[rows, hs], ones], axis=1)
            decay = grow_ref[h:h + 1, pl.ds(r0, LANES)][:, 0:1]
            floor = gcol_ref[rows, 2 * M_HEADS + h:2 * M_HEADS + h + 1]

            s = jnp.dot(q_ref[rows, hs], kt_ref[hs, rows], preferred_element_type=F32)
            p = s.astype(BF16) * d_ref[rows, hs]
            acc = (jnp.dot(p, vaug, preferred_element_type=F32)
                   + jnp.dot(qi_ref[rows, hs], c_ref[h].astype(BF16), preferred_element_type=F32))
            den = acc[:, d:d + 1]
            y_ref[rows, hs] = (acc[:, 0:d] * (1.0 / jnp.maximum(jnp.abs(den), floor))).astype(BF16)
            c_ref[h] = decay * c_ref[h] + jnp.dot(kwt_ref[hs, rows], vaug, preferred_element_type=F32)
        return carry

    lax.fori_loop(0, nchunk, chunk, 0)


def _mlstm_call(q, qi, kt, kwt, v, dmat, grow, gcol, batch, seq):
    tb = min(MLSTM_TB, seq)
    L = MLSTM_L
    nb = seq // tb
    T = batch * seq
    d = M_HEAD_DIM
    tok = lambda w: pl.BlockSpec((tb, w), lambda b, c: (b * nb + c, 0))
    tok_t = lambda r: pl.BlockSpec((r, tb), lambda b, c: (0, b * nb + c))
    return pl.pallas_call(
        functools.partial(_mlstm_kernel, L=L, nchunk=tb // L),
        grid=(batch, nb),
        in_specs=[tok(M_WIDTH), tok(M_WIDTH), tok_t(M_WIDTH), tok_t(M_WIDTH), tok(M_WIDTH), tok(M_WIDTH),
                  tok_t(GATE_ROWS), tok(LANES)],
        out_specs=tok(M_WIDTH),
        out_shape=jax.ShapeDtypeStruct((T, M_WIDTH), BF16),
        scratch_shapes=[pltpu.VMEM((M_HEADS, d, d + LANES), F32)],
        compiler_params=_params(("arbitrary", "arbitrary")), name="mlstm",
    )(q, qi, kt, kwt, v, dmat, grow, gcol)


def _swa_kernel(sink_ref, q_ref, kv_ref, kvp_ref, y_ref, *, nblk):
    first_step = pl.program_id(1) == 0
    W = WINDOW
    row = lax.broadcasted_iota(jnp.int32, (W, W), 0)
    col = lax.broadcasted_iota(jnp.int32, (W, W), 1)
    tri = col <= row
    lane = lax.broadcasted_iota(jnp.int32, (2 * W, LANES), 1)
    lo = lane < A_HEAD_DIM
    out_lo = lax.broadcasted_iota(jnp.int32, (W, LANES), 1) < A_HEAD_DIM
    pairs = A_GROUP // 2

    for j in range(nblk):
        qrows = slice(j * W, (j + 1) * W)
        if j == 0:
            band = jnp.concatenate([kvp_ref[...], kv_ref[0:W, :]], axis=0)
            prev_bias = jnp.where(first_step, -jnp.inf, 0.0)
        else:
            band = kv_ref[(j - 1) * W:(j + 1) * W, :]
            prev_bias = 0.0
        for g in range(A_KV_HEADS):
            kd = band[:, g * LANES:(g + 1) * LANES]
            vd = band[:, (A_KV_HEADS + g) * LANES:(A_KV_HEADS + g + 1) * LANES]
            zero = jnp.zeros_like(kd)
            k_par = (jnp.where(lo, kd, zero), jnp.where(lo, zero, kd))
            v_par = (jnp.where(lo, vd, zero), jnp.where(lo, zero, vd))
            q4 = jnp.concatenate(
                [q_ref[qrows, (g * pairs + p) * LANES:(g * pairs + p + 1) * LANES] for p in range(pairs)],
                axis=0)
            outs = []
            rinv = []
            for par in range(2):
                s = lax.dot_general(q4, k_par[par], (((1,), (1,)), ((), ())),
                                    preferred_element_type=F32)
                pb_list = []
                rv = []
                for p in range(pairs):
                    head = g * A_GROUP + 2 * p + par
                    sp = s[p * W:(p + 1) * W, :]
                    logit = jnp.where(tri, sp[:, W:], sp[:, :W] + prev_bias)
                    sink = sink_ref[head]
                    mx = jnp.maximum(jnp.max(logit, axis=1, keepdims=True), sink)
                    e = jnp.exp(logit - mx)
                    denom = jnp.sum(e, axis=1, keepdims=True) + jnp.exp(sink - mx)
                    rv.append(1.0 / denom)
                    pb_list.append(jnp.concatenate(
                        [jnp.where(tri, 0.0, e), jnp.where(tri, e, 0.0)], axis=1).astype(BF16))
                pb = jnp.concatenate(pb_list, axis=0)
                outs.append(jnp.dot(pb, v_par[par], preferred_element_type=F32))
                rinv.append(rv)
            for p in range(pairs):
                scale = jnp.where(out_lo, rinv[0][p], rinv[1][p])
                o_pair = (outs[0][p * W:(p + 1) * W, :] + outs[1][p * W:(p + 1) * W, :]) * scale
                y_ref[qrows, (g * pairs + p) * LANES:(g * pairs + p + 1) * LANES] = o_pair.astype(BF16)


def _swa_call(sinks, aq, akv, batch, seq):
    tq = min(SWA_TQ, seq)
    nq = seq // tq
    nblk = tq // WINDOW
    T = batch * seq
    return pl.pallas_call(
        functools.partial(_swa_kernel, nblk=nblk),
        grid=(batch, nq),
        in_specs=[pl.BlockSpec(memory_space=pltpu.SMEM),
                  pl.BlockSpec((tq, A_WIDTH), lambda b, s: (b * nq + s, 0)),
                  pl.BlockSpec((tq, 4 * LANES), lambda b, s: (b * nq + s, 0)),
                  pl.BlockSpec((WINDOW, 4 * LANES),
                               lambda b, s: (jnp.maximum((b * nq + s) * nblk - 1, 0), 0))],
        out_specs=pl.BlockSpec((tq, A_WIDTH), lambda b, s: (b * nq + s, 0)),
        out_shape=jax.ShapeDtypeStruct((T, A_WIDTH), BF16),
        compiler_params=_params(("arbitrary", "arbitrary")), name="swa",
    )(sinks, aq, akv, akv)


def _merge_ffn_kernel(x_ref, ht_ref, o_ref, ya_ref, mg_ref, ng_ref, wbm_ref, wba_ref, wout_ref, g2_ref,
                      wfi_ref, wfo_ref, out_ref):
    ym = []
    for h in range(M_HEADS):
        hs = slice(h * M_HEAD_DIM, (h + 1) * M_HEAD_DIM)
        hm = o_ref[:, hs].astype(F32) * ht_ref[:, hs].astype(F32)
        ms = jnp.mean(hm * hm, axis=-1, keepdims=True)
        ym.append((hm * lax.rsqrt(ms + EPS) * ng_ref[:, hs]).astype(BF16))
    m1 = jnp.dot(jnp.concatenate(ym, axis=1), wbm_ref[...], preferred_element_type=F32)
    m2 = jnp.dot(ya_ref[...], wba_ref[...], preferred_element_type=F32)
    merged = mg_ref[:, 0:D_MODEL].astype(F32) * m1 + mg_ref[:, D_MODEL:2 * D_MODEL].astype(F32) * m2
    x1 = x_ref[...] + jnp.dot(merged.astype(BF16), wout_ref[...], preferred_element_type=F32)
    ms = jnp.mean(x1 * x1, axis=-1, keepdims=True)
    h2 = (x1 * lax.rsqrt(ms + EPS) * g2_ref[...]).astype(BF16)
    def gate_up(j):
        return jnp.dot(h2, wfi_ref[:, 2 * j * FFN_CHUNK:2 * (j + 1) * FFN_CHUNK], preferred_element_type=F32)

    nchunk = D_FF // FFN_CHUNK
    acc = x1
    gu = gate_up(0)
    for j in range(nchunk):
        gu_next = gate_up(j + 1) if j + 1 < nchunk else None
        g = gu[:, 0:FFN_CHUNK]
        a = (g * _sigmoid(g) * gu[:, FFN_CHUNK:2 * FFN_CHUNK]).astype(BF16)
        acc = acc + jnp.dot(a, wfo_ref[j * FFN_CHUNK:(j + 1) * FFN_CHUNK, :], preferred_element_type=F32)
        gu = gu_next
    out_ref[...] = acc


def _merge_ffn_call(x2, ht, mo, ya, mg, ng, wbm, wba, wout, g2, wfi, wfo):
    T = x2.shape[0]
    tm = min(FFN_TM, T)
    tok = lambda w: pl.BlockSpec((tm, w), lambda i: (i, 0))
    return pl.pallas_call(
        _merge_ffn_kernel,
        grid=(T // tm,),
        in_specs=[tok(D_MODEL), tok(M_WIDTH), tok(M_WIDTH), tok(A_WIDTH), tok(2 * D_MODEL),
                  _const_spec(ng.shape), _const_spec(wbm.shape), _const_spec(wba.shape),
                  _const_spec(wout.shape), _const_spec(g2.shape), _const_spec(wfi.shape),
                  _const_spec(wfo.shape)],
        out_specs=tok(D_MODEL),
        out_shape=jax.ShapeDtypeStruct((T, D_MODEL), F32),
        compiler_params=_params(("arbitrary",)), name="merge_ffn",
    )(x2, ht, mo, ya, mg, ng, wbm, wba, wout, g2, wfi, wfo)


def _rope_tables(seq):
    half = ROPE_DIM // 2
    inv_freq = ROPE_THETA ** (-jnp.arange(half, dtype=F32) * (2.0 / ROPE_DIM))
    ang = jnp.arange(seq).astype(F32)[:, None] * inv_freq[None, :]
    cos, sin = jnp.cos(ang), jnp.sin(ang)
    pad = A_HEAD_DIM - ROPE_DIM
    cos_h = jnp.concatenate([cos, cos, jnp.ones((seq, pad), F32)], axis=1)
    sin_h = jnp.concatenate([-sin, sin, jnp.zeros((seq, pad), F32)], axis=1)
    return jnp.tile(cos_h, (1, LANES // A_HEAD_DIM)), jnp.tile(sin_h, (1, LANES // A_HEAD_DIM))


def _interleave_gate_up(w):
    n = D_FF // FFN_CHUNK
    return w.reshape(D_MODEL, 2, n, FFN_CHUNK).transpose(0, 2, 1, 3).reshape(D_MODEL, 2 * D_FF)


def kernel(x, norm1_g, w_in, conv_w, conv_b, b_mgate, m_norm_g, q_norm_g, k_norm_g, sinks, b_merge,
           w_branch_m, w_branch_a, w_out, norm2_g, w_ffn_in, w_ffn_out):
    batch, seq, _ = x.shape
    depth = w_in.shape[0]
    cos_t, sin_t = _rope_tables(seq)
    blk = np.arange(LANES) // A_HEAD_DIM
    bd = jnp.asarray(blk[:, None] == blk[None, :], BF16)
    x2 = x.reshape(batch * seq, D_MODEL)
    o_gate = 4 * M_WIDTH
    o_aq = o_gate + 2 * M_HEADS
    o_ak = o_aq + A_WIDTH
    o_mg = o_ak + 2 * A_KV_HEADS * A_HEAD_DIM
    for l in range(depth):
        w = w_in[l]
        wgt = jnp.zeros((GATE_ROWS, D_MODEL), F32).at[0:2 * M_HEADS].set(w[:, o_gate:o_aq].T)
        bg = jnp.zeros((GATE_ROWS, 1), F32).at[0:2 * M_HEADS, 0].set(b_mgate[l])
        mq, mqi, mkt, mkwt, mv, mo, dmat, grow, gcol, aq, akv, mg = _proj_call(
            x2, norm1_g[l][None, :], w[:, 0:2 * M_WIDTH].astype(BF16),
            w[:, 2 * M_WIDTH:o_gate].astype(BF16), wgt.astype(BF16),
            w[:, o_aq:o_ak].astype(BF16), w[:, o_ak:o_mg].astype(BF16), w[:, o_mg:].astype(BF16),
            conv_w[l], conv_b[l][None, :], bg,
            jnp.tile(q_norm_g[l], LANES // A_HEAD_DIM)[None, :],
            jnp.tile(k_norm_g[l], LANES // A_HEAD_DIM)[None, :],
            cos_t, sin_t, bd, b_merge[l][None, :], batch, seq)
        ht = _mlstm_call(mq, mqi, mkt, mkwt, mv, dmat, grow, gcol, batch, seq)
        ya = _swa_call(sinks[l], aq, akv, batch, seq)
        x2 = _merge_ffn_call(
            x2, ht, mo, ya, mg, m_norm_g[l][None, :], w_branch_m[l].astype(BF16), w_branch_a[l].astype(BF16),
            w_out[l].astype(BF16), norm2_g[l][None, :], _interleave_gate_up(w_ffn_in[l]).astype(BF16),
            w_ffn_out[l].astype(BF16))
    return x2.reshape(batch, seq, D_MODEL)
```

```python
import functools

import numpy as np
import jax
import jax.numpy as jnp
from jax import lax
from jax.experimental import pallas as pl
from jax.experimental.pallas import tpu as pltpu

F32 = jnp.float32
BF16 = jnp.bfloat16

D_MODEL = 1024
M_HEADS = 4
M_HEAD_DIM = 256
M_WIDTH = M_HEADS * M_HEAD_DIM
CONV_WIDTH = 4
A_HEAD_DIM = 64
A_HEADS = 16
A_KV_HEADS = 2
A_GROUP = A_HEADS // A_KV_HEADS
A_WIDTH = A_HEADS * A_HEAD_DIM
WINDOW = 128
ROPE_THETA = 500000.0
ROPE_DIM = A_HEAD_DIM // 4
D_FF = 2816
EPS = 1e-6

LANES = 128
MXU_N = 256
CARRY_ROWS = 8
GATE_ROWS = 16
VMEM_LIMIT = 56 * 1024 * 1024

PROJ_TM = 512
PROJ_CHUNK = 2 * MXU_N
MLSTM_TB = 1024
MLSTM_L = 256
SWA_TQ = 512
FFN_TM = 256
FFN_CHUNK = 256


def _const_spec(shape):
    zeros = (0,) * len(shape)
    return pl.BlockSpec(shape, lambda *_: zeros, pipeline_mode=pl.Buffered(1))


def _params(semantics):
    return pltpu.CompilerParams(dimension_semantics=semantics, vmem_limit_bytes=VMEM_LIMIT)


def _sigmoid(z):
    return 1.0 / (1.0 + jnp.exp(-z))


def _norm_rope(a, gain, cos, sin, bd):
    ss = jnp.dot((a * a).astype(BF16), bd, preferred_element_type=F32)
    an = a * lax.rsqrt(ss * (1.0 / A_HEAD_DIM) + EPS) * gain
    lane = lax.broadcasted_iota(jnp.int32, an.shape, 1) % A_HEAD_DIM
    half = ROPE_DIM // 2
    partner = jnp.where(lane < half, pltpu.roll(an, LANES - half, 1), pltpu.roll(an, half, 1))
    return an * cos + partner * sin


def _dup_heads(t, first):
    lane = lax.broadcasted_iota(jnp.int32, t.shape, 1)
    r = pltpu.roll(t, A_HEAD_DIM, 1)
    if first:
        return jnp.where(lane < A_HEAD_DIM, t, r)
    return jnp.where(lane < A_HEAD_DIM, r, t)


def _chunk_scan(t, pos, op, identity):
    shift = 1
    while shift < MLSTM_L:
        t = op(t, jnp.where(pos >= shift, pltpu.roll(t, shift, 1), identity))
        shift *= 2
    return t


def _proj_kernel(x_ref, g1_ref, wqk_ref, wvo_ref, wgt_ref, waq_ref, wakv_ref, wmg_ref,
                 cw_ref, cb_ref, bg_ref, qg_ref, kg_ref, cos_ref, sin_ref, bd_ref,
                 q_ref, qi_ref, kt_ref, kwt_ref, v_ref, o_ref, d_ref, grow_ref, gcol_ref,
                 aq_ref, akv_ref, mg_ref, pbuf, mc_ref, ybuf, *, tm):
    @pl.when(pl.program_id(1) == 0)
    def _():
        pbuf[:, 0:CARRY_ROWS, :] = jnp.zeros((pbuf.shape[0], CARRY_ROWS, LANES), F32)
        mc_ref[...] = jnp.zeros(mc_ref.shape, F32)

    x = x_ref[...]
    ms = jnp.mean(x * x, axis=-1, keepdims=True)
    h = (x * lax.rsqrt(ms + EPS) * g1_ref[...]).astype(BF16)

    gt = lax.dot_general(wgt_ref[...], h, (((1,), (1,)), ((), ())), preferred_element_type=F32)
    gt = gt + bg_ref[...]
    log_f = jnp.minimum(gt, 0.0) - jnp.log(1.0 + jnp.exp(-jnp.abs(gt)))
    lane = lax.broadcasted_iota(jnp.int32, gt.shape, 1)
    pos = lane % MLSTM_L
    b = pltpu.roll(_chunk_scan(log_f, pos, jnp.add, 0.0), GATE_ROWS - M_HEADS, 0)
    a = gt - b
    cm = _chunk_scan(a, pos, jnp.maximum, -jnp.inf)
    m_cur = mc_ref[:, 0:1]
    m_prev = jnp.zeros_like(gt)
    mu_last = jnp.zeros_like(gt)
    for c in range(tm // MLSTM_L):
        last = (c + 1) * MLSTM_L - 1
        in_chunk = lane // MLSTM_L == c
        mu_last_c = jnp.maximum(cm[:, last:last + 1], m_cur)
        m_prev = jnp.where(in_chunk, m_cur, m_prev)
        mu_last = jnp.where(in_chunk, mu_last_c, mu_last)
        m_cur = b[:, last:last + 1] + mu_last_c
    mc_ref[...] = jnp.broadcast_to(m_cur, mc_ref.shape)
    mu = jnp.maximum(cm, m_prev)
    inter = jnp.exp(m_prev - mu)
    floor = jnp.exp(-(b + mu))
    e_last = jnp.exp(a - mu_last)
    decay = jnp.exp(m_prev - mu_last)
    row = lax.broadcasted_iota(jnp.int32, gt.shape, 0)
    stacked = jnp.where(row < M_HEADS, mu,
                        jnp.where(row < 2 * M_HEADS, pltpu.roll(inter, M_HEADS, 0),
                                  jnp.where(row < 3 * M_HEADS, pltpu.roll(floor, 2 * M_HEADS, 0), 0.0)))
    gcol = jnp.concatenate([stacked, jnp.zeros((LANES - GATE_ROWS, tm), F32)], axis=0).T
    gcol_ref[...] = gcol
    grow_ref[...] = decay

    tri = (lax.broadcasted_iota(jnp.int32, (MLSTM_L, MLSTM_L), 1)
           <= lax.broadcasted_iota(jnp.int32, (MLSTM_L, MLSTM_L), 0))
    for hd in range(M_HEADS):
        for c in range(tm // MLSTM_L):
            rs = slice(c * MLSTM_L, (c + 1) * MLSTM_L)
            dmat = jnp.where(tri, jnp.exp(a[hd:hd + 1, rs] - gcol[rs, hd:hd + 1]), 0.0)
            d_ref[rs, hd * M_HEAD_DIM:(hd + 1) * M_HEAD_DIM] = dmat.astype(BF16)

    base = CARRY_ROWS
    d = M_HEAD_DIM

    def chunk_dot(w_ref, c):
        return lambda: jnp.dot(h, w_ref[:, c * PROJ_CHUNK:(c + 1) * PROJ_CHUNK], preferred_element_type=F32)

    def qk_epilogue(c, acc):
        for g in range(PROJ_CHUNK // LANES):
            pbuf[c * (PROJ_CHUNK // LANES) + g, base:base + tm, :] = acc[:, g * LANES:(g + 1) * LANES]
        for sub in range(PROJ_CHUNK // d):
            hd = (c * PROJ_CHUNK) // d + sub
            hs = slice(hd * d, (hd + 1) * d)
            for g in range(d // LANES):
                pg = hd * (d // LANES) + g
                ls = slice(hd * d + g * LANES, hd * d + (g + 1) * LANES)
                cw = cw_ref[:, ls]
                cb = cb_ref[:, ls]
                taps = {e: pbuf[pg, pl.ds(base + e, tm // 8, stride=8), :] for e in range(1 - CONV_WIDTH, 8)}
                for e in range(8):
                    y = cb + cw[CONV_WIDTH - 1:CONV_WIDTH] * taps[e]
                    for j in range(1, CONV_WIDTH):
                        y = y + cw[CONV_WIDTH - 1 - j:CONV_WIDTH - j] * taps[e - j]
                    ybuf[g, pl.ds(e, tm // 8, stride=8), :] = y * _sigmoid(y)
                pbuf[pg, 0:CARRY_ROWS, :] = pbuf[pg, tm:tm + CARRY_ROWS, :]
            y = jnp.concatenate([ybuf[g] for g in range(d // LANES)], axis=1)
            if hd < M_HEADS:
                q_ref[:, hs] = y.astype(BF16)
                qi_ref[:, hs] = (y * gcol[:, M_HEADS + hd:M_HEADS + hd + 1]).astype(BF16)
            else:
                kh = hd - M_HEADS
                yt = (y * (d ** -0.5)).T
                kt_ref[kh * d:(kh + 1) * d, :] = yt.astype(BF16)
                kwt_ref[kh * d:(kh + 1) * d, :] = (yt * e_last[kh:kh + 1, :]).astype(BF16)

    def vo_epilogue(c, r):
        if c * PROJ_CHUNK < M_WIDTH:
            v_ref[:, c * PROJ_CHUNK:(c + 1) * PROJ_CHUNK] = r.astype(BF16)
        else:
            o_ref[:, c * PROJ_CHUNK - M_WIDTH:(c + 1) * PROJ_CHUNK - M_WIDTH] = r.astype(BF16)

    cos = cos_ref[...]
    sin = sin_ref[...]
    bd = bd_ref[...]
    qg = qg_ref[...] * (A_HEAD_DIM ** -0.5)

    def aq_epilogue(c, r):
        for hf in range(PROJ_CHUNK // LANES):
            ls = slice(c * PROJ_CHUNK + hf * LANES, c * PROJ_CHUNK + (hf + 1) * LANES)
            aq_ref[:, ls] = _norm_rope(r[:, hf * LANES:(hf + 1) * LANES], qg, cos, sin, bd).astype(BF16)

    def akv_epilogue(kv):
        kn = _norm_rope(kv[:, 0:LANES], kg_ref[...], cos, sin, bd)
        vv = kv[:, LANES:2 * LANES]
        akv_ref[:, 0 * LANES:1 * LANES] = _dup_heads(kn, True).astype(BF16)
        akv_ref[:, 1 * LANES:2 * LANES] = _dup_heads(kn, False).astype(BF16)
        akv_ref[:, 2 * LANES:3 * LANES] = _dup_heads(vv, True).astype(BF16)
        akv_ref[:, 3 * LANES:4 * LANES] = _dup_heads(vv, False).astype(BF16)

    def mg_epilogue(c, r):
        mg_ref[:, c * PROJ_CHUNK:(c + 1) * PROJ_CHUNK] = r.astype(BF16)

    P = functools.partial
    qk = [(chunk_dot(wqk_ref, c), P(qk_epilogue, c)) for c in range(2 * M_WIDTH // PROJ_CHUNK)]
    vo = [(chunk_dot(wvo_ref, c), P(vo_epilogue, c)) for c in range(2 * M_WIDTH // PROJ_CHUNK)]
    aq = [(chunk_dot(waq_ref, c), P(aq_epilogue, c)) for c in range(A_WIDTH // PROJ_CHUNK)]
    mg = [(chunk_dot(wmg_ref, c), P(mg_epilogue, c)) for c in range(2 * D_MODEL // PROJ_CHUNK)]
    akv = [(lambda: jnp.dot(h, wakv_ref[...], preferred_element_type=F32), akv_epilogue)]
    stages = [qk[0], vo[0], qk[1], vo[1], qk[2], vo[2], qk[3], vo[3],
              aq[0], mg[0], aq[1], mg[1], akv[0], mg[2], mg[3]]
    pending = stages[0][0]()
    for k, (_, epilogue) in enumerate(stages):
        following = stages[k + 1][0]() if k + 1 < len(stages) else None
        epilogue(pending)
        pending = following


def _proj_call(x2, g1, wqk, wvo, wgt, waq, wakv, wmg, cw, cb, bg, qg, kg, cos_t, sin_t, bd, batch, seq):
    tm = min(PROJ_TM, seq)
    ns = seq // tm
    T = batch * seq
    tok = lambda w: pl.BlockSpec((tm, w), lambda b, s: (b * ns + s, 0))
    in_specs = [
        tok(D_MODEL), _const_spec(g1.shape), _const_spec(wqk.shape), _const_spec(wvo.shape),
        _const_spec(wgt.shape), _const_spec(waq.shape), _const_spec(wakv.shape), _const_spec(wmg.shape),
        _const_spec(cw.shape), _const_spec(cb.shape), _const_spec(bg.shape), _const_spec(qg.shape),
        _const_spec(kg.shape),
        pl.BlockSpec((tm, LANES), lambda b, s: (s, 0)), pl.BlockSpec((tm, LANES), lambda b, s: (s, 0)),
        _const_spec(bd.shape),
    ]
    tok_t = lambda r: pl.BlockSpec((r, tm), lambda b, s: (0, b * ns + s))
    rowmajor = lambda w, dt=BF16: (jax.ShapeDtypeStruct((T, w), dt), tok(w))
    colmajor = lambda r, dt=BF16: (jax.ShapeDtypeStruct((r, T), dt), tok_t(r))
    outs = [rowmajor(M_WIDTH), rowmajor(M_WIDTH), colmajor(M_WIDTH), colmajor(M_WIDTH),
            rowmajor(M_WIDTH), rowmajor(M_WIDTH), rowmajor(M_WIDTH),
            colmajor(GATE_ROWS, F32), rowmajor(LANES, F32),
            rowmajor(A_WIDTH), rowmajor(4 * LANES), rowmajor(2 * D_MODEL)]
    out_shape = [o[0] for o in outs]
    out_specs = [o[1] for o in outs]
    return pl.pallas_call(
        functools.partial(_proj_kernel, tm=tm),
        grid=(batch, ns), in_specs=in_specs, out_specs=out_specs, out_shape=out_shape,
        scratch_shapes=[pltpu.VMEM((2 * M_WIDTH // LANES, tm + CARRY_ROWS, LANES), F32),
                        pltpu.VMEM((GATE_ROWS, LANES), F32),
                        pltpu.VMEM((M_HEAD_DIM // LANES, tm, LANES), F32)],
        compiler_params=_params(("arbitrary", "arbitrary")), name="proj",
    )(x2, g1, wqk, wvo, wgt, waq, wakv, wmg, cw, cb, bg, qg, kg, cos_t, sin_t, bd)


def _mlstm_kernel(q_ref, qi_ref, kt_ref, kwt_ref, v_ref, d_ref, grow_ref, gcol_ref, y_ref, c_ref, *, L, nchunk):
    d = M_HEAD_DIM

    @pl.when(pl.program_id(1) == 0)
    def _():
        c_ref[...] = jnp.zeros(c_ref.shape, F32)

    ones = jnp.ones((L, LANES), BF16)

    def chunk(ci, carry):
        r0 = pl.multiple_of(ci * L, L)
        rows = pl.ds(r0, L)
        for h in range(M_HEADS):
            hs = slice(h * d, (h + 1) * d)
            vaug = jnp.concatenate([v_ref[rows, hs], ones], axis=1)
            decay = grow_ref[h:h + 1, pl.ds(r0, LANES)][:, 0:1]
            floor = gcol_ref[rows, 2 * M_HEADS + h:2 * M_HEADS + h + 1]

            s = jnp.dot(q_ref[rows, hs], kt_ref[hs, rows], preferred_element_type=F32)
            p = s.astype(BF16) * d_ref[rows, hs]
            acc = (jnp.dot(p, vaug, preferred_element_type=F32)
                   + jnp.dot(qi_ref[rows, hs], c_ref[h].astype(BF16), preferred_element_type=F32))
            den = acc[:, d:d + 1]
            y_ref[rows, hs] = (acc[:, 0:d] * (1.0 / jnp.maximum(jnp.abs(den), floor))).astype(BF16)
            c_ref[h] = decay * c_ref[h] + jnp.dot(kwt_ref[hs, rows], vaug, preferred_element_type=F32)
        return carry

    lax.fori_loop(0, nchunk, chunk, 0)


def _mlstm_call(q, qi, kt, kwt, v, dmat, grow, gcol, batch, seq):
    tb = min(MLSTM_TB, seq)
    L = MLSTM_L
    nb = seq // tb
    T = batch * seq
    d = M_HEAD_DIM
    tok = lambda w: pl.BlockSpec((tb, w), lambda b, c: (b * nb + c, 0))
    tok_t = lambda r: pl.BlockSpec((r, tb), lambda b, c: (0, b * nb + c))
    return pl.pallas_call(
        functools.partial(_mlstm_kernel, L=L, nchunk=tb // L),
        grid=(batch, nb),
        in_specs=[tok(M_WIDTH), tok(M_WIDTH), tok_t(M_WIDTH), tok_t(M_WIDTH), tok(M_WIDTH), tok(M_WIDTH),
                  tok_t(GATE_ROWS), tok(LANES)],
        out_specs=tok(M_WIDTH),
        out_shape=jax.ShapeDtypeStruct((T, M_WIDTH), BF16),
        scratch_shapes=[pltpu.VMEM((M_HEADS, d, d + LANES), F32)],
        compiler_params=_params(("arbitrary", "arbitrary")), name="mlstm",
    )(q, qi, kt, kwt, v, dmat, grow, gcol)


def _swa_kernel(sink_ref, q_ref, kv_ref, kvp_ref, y_ref, *, nblk):
    first_step = pl.program_id(1) == 0
    W = WINDOW
    row = lax.broadcasted_iota(jnp.int32, (W, W), 0)
    col = lax.broadcasted_iota(jnp.int32, (W, W), 1)
    tri = col <= row
    lane = lax.broadcasted_iota(jnp.int32, (2 * W, LANES), 1)
    lo = lane < A_HEAD_DIM
    out_lo = lax.broadcasted_iota(jnp.int32, (W, LANES), 1) < A_HEAD_DIM
    pairs = A_GROUP // 2

    for j in range(nblk):
        qrows = slice(j * W, (j + 1) * W)
        if j == 0:
            band = jnp.concatenate([kvp_ref[...], kv_ref[0:W, :]], axis=0)
            prev_bias = jnp.where(first_step, -jnp.inf, 0.0)
        else:
            band = kv_ref[(j - 1) * W:(j + 1) * W, :]
            prev_bias = 0.0
        for g in range(A_KV_HEADS):
            kd = band[:, g * LANES:(g + 1) * LANES]
            vd = band[:, (A_KV_HEADS + g) * LANES:(A_KV_HEADS + g + 1) * LANES]
            zero = jnp.zeros_like(kd)
            k_par = (jnp.where(lo, kd, zero), jnp.where(lo, zero, kd))
            v_par = (jnp.where(lo, vd, zero), jnp.where(lo, zero, vd))
            q4 = jnp.concatenate(
                [q_ref[qrows, (g * pairs + p) * LANES:(g * pairs + p + 1) * LANES] for p in range(pairs)],
                axis=0)
            outs = []
            rinv = []
            for par in range(2):
                s = lax.dot_general(q4, k_par[par], (((1,), (1,)), ((), ())),
                                    preferred_element_type=F32)
                pb_list = []
                rv = []
                for p in range(pairs):
                    head = g * A_GROUP + 2 * p + par
                    sp = s[p * W:(p + 1) * W, :]
                    logit = jnp.where(tri, sp[:, W:], sp[:, :W] + prev_bias)
                    sink = sink_ref[head]
                    mx = jnp.maximum(jnp.max(logit, axis=1, keepdims=True), sink)
                    e = jnp.exp(logit - mx)
                    denom = jnp.sum(e, axis=1, keepdims=True) + jnp.exp(sink - mx)
                    rv.append(1.0 / denom)
                    pb_list.append(jnp.concatenate(
                        [jnp.where(tri, 0.0, e), jnp.where(tri, e, 0.0)], axis=1).astype(BF16))
                pb = jnp.concatenate(pb_list, axis=0)
                outs.append(jnp.dot(pb, v_par[par], preferred_element_type=F32))
                rinv.append(rv)
            for p in range(pairs):
                scale = jnp.where(out_lo, rinv[0][p], rinv[1][p])
                o_pair = (outs[0][p * W:(p + 1) * W, :] + outs[1][p * W:(p + 1) * W, :]) * scale
                y_ref[qrows, (g * pairs + p) * LANES:(g * pairs + p + 1) * LANES] = o_pair.astype(BF16)


def _swa_call(sinks, aq, akv, batch, seq):
    tq = min(SWA_TQ, seq)
    nq = seq // tq
    nblk = tq // WINDOW
    T = batch * seq
    return pl.pallas_call(
        functools.partial(_swa_kernel, nblk=nblk),
        grid=(batch, nq),
        in_specs=[pl.BlockSpec(memory_space=pltpu.SMEM),
                  pl.BlockSpec((tq, A_WIDTH), lambda b, s: (b * nq + s, 0)),
                  pl.BlockSpec((tq, 4 * LANES), lambda b, s: (b * nq + s, 0)),
                  pl.BlockSpec((WINDOW, 4 * LANES),
                               lambda b, s: (jnp.maximum((b * nq + s) * nblk - 1, 0), 0))],
        out_specs=pl.BlockSpec((tq, A_WIDTH), lambda b, s: (b * nq + s, 0)),
        out_shape=jax.ShapeDtypeStruct((T, A_WIDTH), BF16),
        compiler_params=_params(("arbitrary", "arbitrary")), name="swa",
    )(sinks, aq, akv, akv)


def _merge_ffn_kernel(x_ref, ht_ref, o_ref, ya_ref, mg_ref, ng_ref, bm_ref, wbm_ref, wba_ref, wout_ref, g2_ref,
                      wfi_ref, wfo_ref, out_ref, x1_buf):
    @pl.when(pl.program_id(0) == 0)
    def _():
        x1_buf[...] = jnp.zeros(x1_buf.shape, F32)

    x1_prev = x1_buf[...]
    ms_prev = jnp.mean(x1_prev * x1_prev, axis=-1, keepdims=True)
    h2 = (x1_prev * lax.rsqrt(ms_prev + EPS) * g2_ref[...]).astype(BF16)

    st = {}

    def branch_a():
        st["ga_m2"] = (_sigmoid(mg_ref[:, D_MODEL:2 * D_MODEL].astype(F32) + bm_ref[:, D_MODEL:2 * D_MODEL])
                       * jnp.dot(ya_ref[...], wba_ref[...], preferred_element_type=F32))

    def head_norm(h):
        hs = slice(h * M_HEAD_DIM, (h + 1) * M_HEAD_DIM)
        hm = _sigmoid(o_ref[:, hs].astype(F32)) * ht_ref[:, hs].astype(F32)
        ms = jnp.mean(hm * hm, axis=-1, keepdims=True)
        st[h] = (hm * lax.rsqrt(ms + EPS) * ng_ref[:, hs]).astype(BF16)

    def branch_m():
        ym = jnp.concatenate([st[h] for h in range(M_HEADS)], axis=1)
        st["gm"] = _sigmoid(mg_ref[:, 0:D_MODEL].astype(F32) + bm_ref[:, 0:D_MODEL])
        st["m1"] = jnp.dot(ym, wbm_ref[...], preferred_element_type=F32)

    def out_proj():
        merged = (st["gm"] * st["m1"] + st["ga_m2"]).astype(BF16)
        x1_buf[...] = x_ref[...] + jnp.dot(merged, wout_ref[...], preferred_element_type=F32)

    P = functools.partial
    merge_steps = {0: branch_a, 1: P(head_norm, 0), 2: P(head_norm, 1), 3: P(head_norm, 2), 4: P(head_norm, 3),
                   5: branch_m, 8: out_proj}

    def gate_up(j):
        g = jnp.dot(h2, wfi_ref[:, j * FFN_CHUNK:(j + 1) * FFN_CHUNK], preferred_element_type=F32)
        u = jnp.dot(h2, wfi_ref[:, D_FF + j * FFN_CHUNK:D_FF + (j + 1) * FFN_CHUNK],
                    preferred_element_type=F32)
        return g, u

    nchunk = D_FF // FFN_CHUNK
    acc = x1_prev
    gu = gate_up(0)
    for j in range(nchunk):
        gu_next = gate_up(j + 1) if j + 1 < nchunk else None
        g, u = gu
        a = (g * _sigmoid(g) * u).astype(BF16)
        acc = acc + jnp.dot(a, wfo_ref[j * FFN_CHUNK:(j + 1) * FFN_CHUNK, :], preferred_element_type=F32)
        gu = gu_next
        if j in merge_steps:
            merge_steps[j]()
    out_ref[...] = acc


def _merge_ffn_call(x2, ht, mo, ya, mg, ng, bm, wbm, wba, wout, g2, wfi, wfo):
    T = x2.shape[0]
    tm = min(FFN_TM, T)
    nt = T // tm
    tok = lambda w: pl.BlockSpec((tm, w), lambda i: (jnp.minimum(i, nt - 1), 0))
    return pl.pallas_call(
        _merge_ffn_kernel,
        grid=(nt + 1,),
        in_specs=[tok(D_MODEL), tok(M_WIDTH), tok(M_WIDTH), tok(A_WIDTH), tok(2 * D_MODEL),
                  _const_spec(ng.shape), _const_spec(bm.shape), _const_spec(wbm.shape), _const_spec(wba.shape),
                  _const_spec(wout.shape), _const_spec(g2.shape), _const_spec(wfi.shape),
                  _const_spec(wfo.shape)],
        out_specs=pl.BlockSpec((tm, D_MODEL), lambda i: (jnp.maximum(i - 1, 0), 0)),
        out_shape=jax.ShapeDtypeStruct((T, D_MODEL), F32),
        scratch_shapes=[pltpu.VMEM((tm, D_MODEL), F32)],
        compiler_params=_params(("arbitrary",)), name="merge_ffn",
    )(x2, ht, mo, ya, mg, ng, bm, wbm, wba, wout, g2, wfi, wfo)


def _rope_tables(seq):
    half = ROPE_DIM // 2
    inv_freq = ROPE_THETA ** (-jnp.arange(half, dtype=F32) * (2.0 / ROPE_DIM))
    ang = jnp.arange(seq).astype(F32)[:, None] * inv_freq[None, :]
    cos, sin = jnp.cos(ang), jnp.sin(ang)
    pad = A_HEAD_DIM - ROPE_DIM
    cos_h = jnp.concatenate([cos, cos, jnp.ones((seq, pad), F32)], axis=1)
    sin_h = jnp.concatenate([-sin, sin, jnp.zeros((seq, pad), F32)], axis=1)
    return jnp.tile(cos_h, (1, LANES // A_HEAD_DIM)), jnp.tile(sin_h, (1, LANES // A_HEAD_DIM))


def kernel(x, norm1_g, w_in, conv_w, conv_b, b_mgate, m_norm_g, q_norm_g, k_norm_g, sinks, b_merge,
           w_branch_m, w_branch_a, w_out, norm2_g, w_ffn_in, w_ffn_out):
    batch, seq, _ = x.shape
    depth = w_in.shape[0]
    cos_t, sin_t = _rope_tables(seq)
    blk = np.arange(LANES) // A_HEAD_DIM
    bd = jnp.asarray(blk[:, None] == blk[None, :], BF16)
    x2 = x.reshape(batch * seq, D_MODEL)
    o_gate = 4 * M_WIDTH
    o_aq = o_gate + 2 * M_HEADS
    o_ak = o_aq + A_WIDTH
    o_mg = o_ak + 2 * A_KV_HEADS * A_HEAD_DIM
    for l in range(depth):
        w = w_in[l]
        wgt = jnp.zeros((GATE_ROWS, D_MODEL), F32).at[0:2 * M_HEADS].set(w[:, o_gate:o_aq].T)
        bg = jnp.zeros((GATE_ROWS, 1), F32).at[0:2 * M_HEADS, 0].set(b_mgate[l])
        mq, mqi, mkt, mkwt, mv, mo, dmat, grow, gcol, aq, akv, mg = _proj_call(
            x2, norm1_g[l][None, :], w[:, 0:2 * M_WIDTH].astype(BF16),
            w[:, 2 * M_WIDTH:o_gate].astype(BF16), wgt.astype(BF16),
            w[:, o_aq:o_ak].astype(BF16), w[:, o_ak:o_mg].astype(BF16), w[:, o_mg:].astype(BF16),
            conv_w[l], conv_b[l][None, :], bg,
            jnp.tile(q_norm_g[l], LANES // A_HEAD_DIM)[None, :],
            jnp.tile(k_norm_g[l], LANES // A_HEAD_DIM)[None, :],
            cos_t, sin_t, bd, batch, seq)
        ht = _mlstm_call(mq, mqi, mkt, mkwt, mv, dmat, grow, gcol, batch, seq)
        ya = _swa_call(sinks[l], aq, akv, batch, seq)
        x2 = _merge_ffn_call(
            x2, ht, mo, ya, mg, m_norm_g[l][None, :], b_merge[l][None, :], w_branch_m[l].astype(BF16),
            w_branch_a[l].astype(BF16), w_out[l].astype(BF16), norm2_g[l][None, :],
            w_ffn_in[l].astype(BF16), w_ffn_out[l].astype(BF16))
    return x2.reshape(batch, seq, D_MODEL)
```
